```python
import math
import jax, jax.numpy as jnp
from jax import lax
import numpy as np

D_MODEL = 1024
BATCH = 4
SEQ = 4096
DEPTH = 4

HEAD_DIM = 64
ROPE_DIMS = HEAD_DIM // 4
ROPE_THETA = 500000.0
Q_BLOCK = 128
CONV_W = 3
CONV_WIDTH = 512
DIFF_HEADS = 4
DIFF_V_DIM = 2 * HEAD_DIM
NSA_HEADS = 8
NSA_GROUPS = 2
NSA_HPG = NSA_HEADS // NSA_GROUPS
CMP_BLOCK = 32
CMP_STRIDE = 16
SEL_BLOCK = 64
SEL_TOPK = 16
N_LOCAL_SEL = 2
WINDOW = 512
PHI_HIDDEN = 128
BRANCH_WIDTH = 512
N_BRANCHES = 3
D_FF = 2816
ALPHA = (2 * DEPTH) ** 0.25
BETA = (8 * DEPTH) ** -0.25
LN_EPS = 1e-5
NEG = -1e30
BIG = 1e30

COL_SIZES = (CONV_WIDTH, CONV_WIDTH, CONV_WIDTH,
             DIFF_HEADS * 2 * HEAD_DIM, DIFF_HEADS * 2 * HEAD_DIM, DIFF_HEADS * DIFF_V_DIM,
             NSA_HEADS * HEAD_DIM, 6 * NSA_GROUPS * HEAD_DIM, N_BRANCHES * NSA_HEADS,
             N_BRANCHES * D_MODEL)
D_IN = sum(COL_SIZES)
SPLIT_POINTS = tuple(int(c) for c in np.cumsum(COL_SIZES)[:-1])

kernel_name = 'hybrid_conv_diffattn_nsa_trunk'


def rope_tables(seq):
    pos = jnp.arange(seq, dtype=jnp.float32)
    inv_freq = ROPE_THETA ** (-jnp.arange(0, ROPE_DIMS, 2, dtype=jnp.float32) / ROPE_DIMS)
    ang = pos[:, None] * inv_freq[None, :]
    return jnp.cos(ang), jnp.sin(ang)


def partial_rope(t, cos, sin):
    half = ROPE_DIMS // 2
    c = cos[None, :, None, :].astype(t.dtype)
    s = sin[None, :, None, :].astype(t.dtype)
    t1, t2, rest = t[..., :half], t[..., half:ROPE_DIMS], t[..., ROPE_DIMS:]
    return jnp.concatenate([t1 * c - t2 * s, t1 * s + t2 * c, rest], axis=-1)


def causal_dwconv(x, w):
    c = x.shape[-1]
    return lax.conv_general_dilated(
        x, w[:, None, :].astype(x.dtype), window_strides=(1,),
        padding=[(w.shape[0] - 1, 0)], dimension_numbers=('NWC', 'WIO', 'NWC'),
        feature_group_count=c)


def layer_norm(x, g, b):
    xf = x.astype(jnp.float32)
    mu = jnp.mean(xf, axis=-1, keepdims=True)
    var = jnp.mean(jnp.square(xf - mu), axis=-1, keepdims=True)
    return ((xf - mu) * lax.rsqrt(var + LN_EPS) * g.astype(jnp.float32) + b.astype(jnp.float32)).astype(x.dtype)


def rms_norm(x, g):
    xf = x.astype(jnp.float32)
    ms = jnp.mean(jnp.square(xf), axis=-1, keepdims=True)
    return (xf * lax.rsqrt(ms + LN_EPS) * g.astype(jnp.float32)).astype(x.dtype)


def diff_attention(q, k, v, lam, lam_init, subln_g):
    b, s = q.shape[0], q.shape[1]
    nb = s // Q_BLOCK
    scale = HEAD_DIM ** -0.5
    qb = q.reshape(b, nb, Q_BLOCK, DIFF_HEADS, 2, HEAD_DIM).transpose(1, 0, 2, 3, 4, 5)
    kpos = jnp.arange(s)

    def one_block(args):
        i, q_i = args
        qpos = i * Q_BLOCK + jnp.arange(Q_BLOCK)
        sc = jnp.einsum('bqhmd,bkhmd->bhmqk', q_i, k).astype(jnp.float32) * scale
        sc = jnp.where(kpos[None, :] <= qpos[:, None], sc, -jnp.inf)
        p = jax.nn.softmax(sc, axis=-1)
        a = p[:, :, 0] - lam * p[:, :, 1]
        return jnp.einsum('bhqk,bkhd->bqhd', a.astype(v.dtype), v)

    o = lax.map(one_block, (jnp.arange(nb), qb))
    o = o.transpose(1, 0, 2, 3, 4).reshape(b, s, DIFF_HEADS, DIFF_V_DIM)
    o = rms_norm(o, subln_g) * (1.0 - lam_init)
    return o.reshape(b, s, DIFF_HEADS * DIFF_V_DIM)


def compress_kv(kv, pos_emb, w1, w2):
    b, s = kv.shape[0], kv.shape[1]
    n_cmp = (s - CMP_BLOCK) // CMP_STRIDE + 1
    idx = jnp.arange(n_cmp)[:, None] * CMP_STRIDE + jnp.arange(CMP_BLOCK)[None, :]
    blocks = kv[:, idx] + pos_emb[None, None, :, None, :]
    flat = blocks.transpose(0, 1, 3, 2, 4).reshape(b, n_cmp, NSA_GROUPS, CMP_BLOCK * HEAD_DIM)
    return jax.nn.gelu(flat @ w1, approximate=False) @ w2


def nsa_attention(q, kc, vc, ks, vs, kw, vw, gate_logits, cmp_pos, phi_w1, phi_w2):
    b, s = q.shape[0], q.shape[1]
    scale = HEAD_DIM ** -0.5
    qg = q.reshape(b, s, NSA_GROUPS, NSA_HPG, HEAD_DIM)
    tpos = jnp.arange(s)

    kc_c = compress_kv(kc, cmp_pos[0], phi_w1[0], phi_w2[0])
    vc_c = compress_kv(vc, cmp_pos[1], phi_w1[1], phi_w2[1])
    n_cmp = kc_c.shape[1]
    cstart = jnp.arange(n_cmp) * CMP_STRIDE
    cvalid = (cstart + CMP_BLOCK - 1)[None, :] <= tpos[:, None]
    sc = jnp.einsum('bsghd,bngd->bghsn', qg, kc_c).astype(jnp.float32) * scale
    p_cmp = jax.nn.softmax(jnp.where(cvalid, sc, NEG), axis=-1) * cvalid
    o_cmp = jnp.einsum('bghsn,bngd->bsghd', p_cmp.astype(vc_c.dtype), vc_c)

    n_sel = s // SEL_BLOCK
    sstart = jnp.arange(n_sel) * SEL_BLOCK
    overlap = jnp.maximum(
        jnp.minimum((cstart + CMP_BLOCK)[:, None], (sstart + SEL_BLOCK)[None, :])
        - jnp.maximum(cstart[:, None], sstart[None, :]), 0).astype(jnp.float32) / CMP_BLOCK
    imp = jnp.einsum('bghsn,nj->bgsj', p_cmp, overlap)
    cur = tpos // SEL_BLOCK
    j = jnp.arange(n_sel)
    forced = (j[None, :] == 0) | ((j[None, :] <= cur[:, None]) & (j[None, :] > cur[:, None] - N_LOCAL_SEL))
    future = j[None, :] > cur[:, None]
    imp = jnp.where(forced, BIG, jnp.where(future, NEG, imp))
    top_k = min(SEL_TOPK, n_sel)
    _, sel_idx = lax.top_k(imp, top_k)

    nb = s // Q_BLOCK
    qb = qg.reshape(b, nb, Q_BLOCK, NSA_GROUPS, NSA_HPG, HEAD_DIM).transpose(1, 0, 2, 3, 4, 5)
    idxb = sel_idx.reshape(b, NSA_GROUPS, nb, Q_BLOCK, top_k).transpose(2, 0, 1, 3, 4)
    ks_blk = ks.reshape(b, n_sel, SEL_BLOCK, NSA_GROUPS, HEAD_DIM).transpose(0, 3, 1, 2, 4)
    vs_blk = vs.reshape(b, n_sel, SEL_BLOCK, NSA_GROUPS, HEAD_DIM).transpose(0, 3, 1, 2, 4)
    gather = jax.vmap(jax.vmap(lambda blocks, ix: blocks[ix]))
    kw_p = jnp.pad(kw, ((0, 0), (WINDOW, 0), (0, 0), (0, 0)))
    vw_p = jnp.pad(vw, ((0, 0), (WINDOW, 0), (0, 0), (0, 0)))
    offs = jnp.arange(SEL_BLOCK)
    n_keys = top_k * SEL_BLOCK

    def one_block(args):
        i, q_i, ix = args
        qpos = i * Q_BLOCK + jnp.arange(Q_BLOCK)
        kg = gather(ks_blk, ix).reshape(b, NSA_GROUPS, Q_BLOCK, n_keys, HEAD_DIM)
        vg = gather(vs_blk, ix).reshape(b, NSA_GROUPS, Q_BLOCK, n_keys, HEAD_DIM)
        kpos = (ix[..., None] * SEL_BLOCK + offs).reshape(b, NSA_GROUPS, Q_BLOCK, n_keys)
        sc = jnp.einsum('bqghd,bgqnd->bghqn', q_i, kg).astype(jnp.float32) * scale
        sc = jnp.where((kpos <= qpos[None, None, :, None])[:, :, None], sc, -jnp.inf)
        o_sel = jnp.einsum('bghqn,bgqnd->bqghd', jax.nn.softmax(sc, axis=-1).astype(vg.dtype), vg)
        kwin = lax.dynamic_slice_in_dim(kw_p, i * Q_BLOCK, Q_BLOCK + WINDOW, axis=1)
        vwin = lax.dynamic_slice_in_dim(vw_p, i * Q_BLOCK, Q_BLOCK + WINDOW, axis=1)
        wpos = i * Q_BLOCK - WINDOW + jnp.arange(Q_BLOCK + WINDOW)
        wmask = ((wpos[None, :] <= qpos[:, None]) & (wpos[None, :] > qpos[:, None] - WINDOW)
                 & (wpos[None, :] >= 0))
        sc = jnp.einsum('bqghd,bkgd->bghqk', q_i, kwin).astype(jnp.float32) * scale
        sc = jnp.where(wmask, sc, -jnp.inf)
        o_win = jnp.einsum('bghqk,bkgd->bqghd', jax.nn.softmax(sc, axis=-1).astype(vwin.dtype), vwin)
        return o_sel, o_win

    o_sel, o_win = lax.map(one_block, (jnp.arange(nb), qb, idxb))
    o_sel = o_sel.transpose(1, 0, 2, 3, 4, 5).reshape(b, s, NSA_GROUPS, NSA_HPG, HEAD_DIM)
    o_win = o_win.transpose(1, 0, 2, 3, 4, 5).reshape(b, s, NSA_GROUPS, NSA_HPG, HEAD_DIM)
    g = jax.nn.sigmoid(gate_logits).reshape(b, s, NSA_GROUPS, NSA_HPG, N_BRANCHES)
    o = g[..., 0:1] * o_cmp + g[..., 1:2] * o_sel + g[..., 2:3] * o_win
    return o.reshape(b, s, NSA_HEADS * HEAD_DIM)


def token_mixing(x, w_in, conv_a_w, lam_vec, subln_g, cmp_pos, phi_w1, phi_w2, w_branch, w_o,
                 lam_init, cos, sin):
    b, s = x.shape[0], x.shape[1]
    z = x @ w_in
    a_b, a_c, a_v, d_q, d_k, d_v, n_q, n_kv, n_g, m_g = jnp.split(z, SPLIT_POINTS, axis=-1)
    y_a = a_b * causal_dwconv(a_c * a_v, conv_a_w)
    dq = partial_rope(d_q.reshape(b, s, DIFF_HEADS * 2, HEAD_DIM), cos, sin).reshape(b, s, DIFF_HEADS, 2, HEAD_DIM)
    dk = partial_rope(d_k.reshape(b, s, DIFF_HEADS * 2, HEAD_DIM), cos, sin).reshape(b, s, DIFF_HEADS, 2, HEAD_DIM)
    dv = d_v.reshape(b, s, DIFF_HEADS, DIFF_V_DIM)
    lf = lam_vec.astype(jnp.float32)
    lam = jnp.exp(jnp.dot(lf[0], lf[1])) - jnp.exp(jnp.dot(lf[2], lf[3])) + lam_init
    y_b = diff_attention(dq, dk, dv, lam, lam_init, subln_g)
    nq = partial_rope(n_q.reshape(b, s, NSA_HEADS, HEAD_DIM), cos, sin)
    kv6 = n_kv.reshape(b, s, 6, NSA_GROUPS, HEAD_DIM)
    kc = partial_rope(kv6[:, :, 0], cos, sin)
    ks = partial_rope(kv6[:, :, 2], cos, sin)
    kw = partial_rope(kv6[:, :, 4], cos, sin)
    y_c = nsa_attention(nq, kc, kv6[:, :, 1], ks, kv6[:, :, 3], kw, kv6[:, :, 5], n_g,
                        cmp_pos, phi_w1, phi_w2)
    gates = jax.nn.sigmoid(m_g).reshape(b, s, N_BRANCHES, D_MODEL)
    merged = (gates[:, :, 0] * (y_a @ w_branch[0])
              + gates[:, :, 1] * (y_b @ w_branch[1])
              + gates[:, :, 2] * (y_c @ w_branch[2]))
    return merged @ w_o


def conv_ffn(x, w_up, conv_w, conv_b, w_down):
    h = causal_dwconv(x @ w_up, conv_w) + conv_b
    a, u = jnp.split(h, 2, axis=-1)
    return (jax.nn.gelu(a, approximate=False) * u) @ w_down


def setup_inputs(seed: int = 0) -> dict:
    key = jax.random.key(seed)
    k = jax.random.split(key, 20)

    def nrm(kk, shape, scale):
        return jax.random.normal(kk, shape, jnp.float32) * scale

    return {
        'x': nrm(k[0], (BATCH, SEQ, D_MODEL), 1.0),
        'w_in': nrm(k[1], (DEPTH, D_MODEL, D_IN), D_MODEL ** -0.5),
        'conv_a_w': nrm(k[2], (DEPTH, CONV_W, CONV_WIDTH), CONV_W ** -0.5),
        'diff_lambda': nrm(k[3], (DEPTH, 4, HEAD_DIM), 0.1),
        'diff_subln': 1.0 + nrm(k[4], (DEPTH, DIFF_V_DIM), 0.02),
        'nsa_cmp_pos': nrm(k[5], (DEPTH, 2, CMP_BLOCK, HEAD_DIM), 0.1),
        'nsa_phi_w1': nrm(k[6], (DEPTH, 2, CMP_BLOCK * HEAD_DIM, PHI_HIDDEN), (CMP_BLOCK * HEAD_DIM) ** -0.5),
        'nsa_phi_w2': nrm(k[7], (DEPTH, 2, PHI_HIDDEN, HEAD_DIM), PHI_HIDDEN ** -0.5),
        'w_branch': nrm(k[8], (DEPTH, N_BRANCHES, BRANCH_WIDTH, D_MODEL), BRANCH_WIDTH ** -0.5),
        'w_o': nrm(k[9], (DEPTH, D_MODEL, D_MODEL), BETA * D_MODEL ** -0.5),
        'ln1_g': 1.0 + nrm(k[10], (DEPTH, D_MODEL), 0.02),
        'ln1_b': nrm(k[11], (DEPTH, D_MODEL), 0.02),
        'ffn_w_up': nrm(k[12], (DEPTH, D_MODEL, 2 * D_FF), D_MODEL ** -0.5),
        'ffn_conv_w': nrm(k[13], (DEPTH, CONV_W, 2 * D_FF), CONV_W ** -0.5),
        'ffn_conv_b': nrm(k[14], (DEPTH, 2 * D_FF), 0.02),
        'ffn_w_down': nrm(k[15], (DEPTH, D_FF, D_MODEL), BETA * D_FF ** -0.5),
        'ln2_g': 1.0 + nrm(k[16], (DEPTH, D_MODEL), 0.02),
        'ln2_b': nrm(k[17], (DEPTH, D_MODEL), 0.02),
    }


def reference(x, w_in, conv_a_w, diff_lambda, diff_subln, nsa_cmp_pos, nsa_phi_w1, nsa_phi_w2,
              w_branch, w_o, ln1_g, ln1_b, ffn_w_up, ffn_conv_w, ffn_conv_b, ffn_w_down,
              ln2_g, ln2_b):
    cos, sin = rope_tables(x.shape[1])
    for l in range(DEPTH):
        lam_init = 0.8 - 0.6 * math.exp(-0.3 * l)
        mix = token_mixing(x, w_in[l], conv_a_w[l], diff_lambda[l], diff_subln[l], nsa_cmp_pos[l],
                           nsa_phi_w1[l], nsa_phi_w2[l], w_branch[l], w_o[l], lam_init, cos, sin)
        x = layer_norm(ALPHA * x + mix, ln1_g[l], ln1_b[l])
        ffn = conv_ffn(x, ffn_w_up[l], ffn_conv_w[l], ffn_conv_b[l], ffn_w_down[l])
        x = layer_norm(ALPHA * x + ffn, ln2_g[l], ln2_b[l])
    return x
```

```python
import functools
import math

import jax
import jax.numpy as jnp
import numpy as np
from jax import lax
from jax.experimental import pallas as pl
from jax.experimental.pallas import tpu as pltpu

F32 = jnp.float32
BF16 = jnp.bfloat16

D_MODEL = 1024
DEPTH = 4
HEAD_DIM = 64
ROPE_DIMS = HEAD_DIM // 4
ROPE_THETA = 500000.0
CONV_W = 3
CONV_WIDTH = 512
DIFF_HEADS = 4
DIFF_V_DIM = 2 * HEAD_DIM
NSA_HEADS = 8
NSA_GROUPS = 2
NSA_HPG = NSA_HEADS // NSA_GROUPS
CMP_BLOCK = 32
CMP_STRIDE = 16
SEL_BLOCK = 64
SEL_TOPK = 16
N_LOCAL_SEL = 2
WINDOW = 512
PHI_HIDDEN = 128
N_BRANCHES = 3
D_FF = 2816
ALPHA = (2 * DEPTH) ** 0.25
LN_EPS = 1e-5
NEG = -1e30
BIG = 1e30
SCALE = HEAD_DIM ** -0.5

LANES = 128
SUBLANES = 8
VMEM_LIMIT = 48 * 1024 * 1024

ZF_AB, ZF_AC, ZF_AV = 0, 512, 1024
ZF_NG = 1536
ZF_KCVC = 1792
ZF_MG = 2048
NF = 5120
ZH_DQ, ZH_DK, ZH_NQ = 0, 512, 1024
ZH_KSVS = 1536
ZH_KWVW = 1792
ZH_DV = 2048
NH = 2560

PROJ_TN = 256
ATT_T = 256


def _cparams(sem):
    return pltpu.CompilerParams(dimension_semantics=sem, vmem_limit_bytes=VMEM_LIMIT)


def _in_proj_kernel(x_ref, w_ref, c_ref, s1_ref, s2_ref, o_ref, *, both, first):
    j = pl.program_id(1)
    acc = jnp.dot(x_ref[...].astype(BF16), w_ref[...], preferred_element_type=F32)
    tn = acc.shape[1]
    is_both = jnp.logical_and(j >= both[0], j < both[1])
    is_first = jnp.logical_and(j >= first[0], j < first[1])

    @pl.when(jnp.logical_not(jnp.logical_or(is_both, is_first)))
    def _():
        o_ref[...] = acc.astype(o_ref.dtype)

    @pl.when(jnp.logical_or(is_both, is_first))
    def _():
        lane = lax.broadcasted_iota(jnp.int32, (1, LANES), 1)
        keep = jnp.logical_or(is_both, lane < HEAD_DIM)
        c = jnp.where(keep, c_ref[...], 1.0)
        s1 = jnp.where(keep, s1_ref[...], 0.0)
        s2 = jnp.where(keep, s2_ref[...], 0.0)
        for hh in range(tn // LANES):
            a = acc[:, hh * LANES:(hh + 1) * LANES]
            r = (a * c + pltpu.roll(a, LANES - ROPE_DIMS // 2, 1) * s1
                 + pltpu.roll(a, ROPE_DIMS // 2, 1) * s2)
            o_ref[:, hh * LANES:(hh + 1) * LANES] = r.astype(o_ref.dtype)


def _in_proj(x2, w, ropec, ropes1, ropes2, out_dtype, both, first, seq, tm=512):
    t, k = x2.shape
    n = w.shape[1]
    tn = PROJ_TN
    npos = seq // tm
    kern = functools.partial(_in_proj_kernel, both=both, first=first)
    tab = pl.BlockSpec((tm, LANES), lambda i, j: (i % npos, 0))
    return pl.pallas_call(
        kern,
        grid=(t // tm, n // tn),
        in_specs=[pl.BlockSpec((tm, k), lambda i, j: (i, 0)),
                  pl.BlockSpec((k, tn), lambda i, j: (0, j)),
                  tab, tab, tab],
        out_specs=pl.BlockSpec((tm, tn), lambda i, j: (i, j)),
        out_shape=jax.ShapeDtypeStruct((t, n), out_dtype),
        compiler_params=_cparams(("parallel", "arbitrary")),
        name="in_proj",
    )(x2, w, ropec, ropes1, ropes2)


def _online_init(m_ref, l_ref, acc_ref):
    m_ref[...] = jnp.full(m_ref.shape, NEG, F32)
    l_ref[...] = jnp.zeros(l_ref.shape, F32)
    acc_ref[...] = jnp.zeros(acc_ref.shape, F32)


def _online_update(s, kv, m_ref, l_ref, acc_ref, idx):
    m_prev = m_ref[idx]
    m_new = jnp.maximum(m_prev, jnp.max(s, axis=-1, keepdims=True))
    alpha = jnp.exp(m_prev - m_new)
    p = jnp.exp(s - m_new)
    l_ref[idx] = alpha * l_ref[idx] + jnp.sum(p, axis=-1, keepdims=True)
    acc_ref[idx] = alpha * acc_ref[idx] + jnp.dot(p.astype(kv.dtype), kv, preferred_element_type=F32)
    m_ref[idx] = m_new


def _qk(q, k):
    return lax.dot_general(q, k, (((1,), (1,)), ((), ())), preferred_element_type=F32)


def _tile_iotas(t):
    row = lax.broadcasted_iota(jnp.int32, (t, t), 0)
    col = lax.broadcasted_iota(jnp.int32, (t, t), 1)
    return row, col


def _diff_attn_kernel(q_ref, k_ref, v_ref, lam_ref, g_ref, o_ref, m_ref, l_ref, acc_ref, *, lam_init):
    t = q_ref.shape[0]
    qi = pl.program_id(2)
    lane = lax.broadcasted_iota(jnp.int32, (1, LANES), 1)
    qs = q_ref[...].astype(F32) * SCALE
    q1 = jnp.where(lane < HEAD_DIM, qs, 0.0).astype(BF16)
    q2 = jnp.where(lane >= HEAD_DIM, qs, 0.0).astype(BF16)
    _online_init(m_ref, l_ref, acc_ref)

    def step(kt, causal):
        start = pl.multiple_of(kt * t, t)
        k = k_ref[pl.ds(start, t), :]
        v = v_ref[pl.ds(start, t), :]
        for mi, qm in enumerate((q1, q2)):
            s = _qk(qm, k)
            if causal:
                row, col = _tile_iotas(t)
                s = jnp.where(col <= row, s, NEG)
            _online_update(s, v, m_ref, l_ref, acc_ref, mi)

    def body(kt, carry):
        step(kt, False)
        return carry

    lax.fori_loop(0, qi, body, 0)
    step(qi, True)

    lf = lam_ref[...]
    lam = (jnp.exp(jnp.sum(lf[0:1] * lf[1:2], axis=1, keepdims=True))
           - jnp.exp(jnp.sum(lf[2:3] * lf[3:4], axis=1, keepdims=True)) + lam_init)
    o = acc_ref[0] / l_ref[0] - lam * (acc_ref[1] / l_ref[1])
    ms = jnp.mean(o * o, axis=-1, keepdims=True)
    o = o * lax.rsqrt(ms + LN_EPS) * g_ref[...] * (1.0 - lam_init)
    o_ref[...] = o.astype(o_ref.dtype)


def _diff_attn(zh3, lam_vec, subln, lam_init):
    b, s, _ = zh3.shape
    t = ATT_T
    kern = functools.partial(_diff_attn_kernel, lam_init=lam_init)
    return pl.pallas_call(
        kern,
        grid=(b, DIFF_HEADS, s // t),
        in_specs=[pl.BlockSpec((None, t, LANES), lambda bi, h, qi: (bi, qi, ZH_DQ // LANES + h)),
                  pl.BlockSpec((None, s, LANES), lambda bi, h, qi: (bi, 0, ZH_DK // LANES + h)),
                  pl.BlockSpec((None, s, LANES), lambda bi, h, qi: (bi, 0, ZH_DV // LANES + h)),
                  pl.BlockSpec((4, HEAD_DIM), lambda bi, h, qi: (0, 0)),
                  pl.BlockSpec((1, DIFF_V_DIM), lambda bi, h, qi: (0, 0))],
        out_specs=pl.BlockSpec((None, t, LANES), lambda bi, h, qi: (bi, qi, h)),
        out_shape=jax.ShapeDtypeStruct((b, s, DIFF_HEADS * DIFF_V_DIM), BF16),
        scratch_shapes=[pltpu.VMEM((2, t, 1), F32), pltpu.VMEM((2, t, 1), F32),
                        pltpu.VMEM((2, t, LANES), F32)],
        compiler_params=_cparams(("parallel", "parallel", "arbitrary")),
        name="diff_attn",
    )(zh3, zh3, zh3, lam_vec, subln)


def _gelu_exact(x):
    return 0.5 * x * (1.0 + lax.erf(x * (2.0 ** -0.5)))


def _compress_kernel(x_ref, w1_ref, pos_ref, w2_ref, o_ref):
    nc = o_ref.shape[0]
    half = CMP_BLOCK // 2
    a = jnp.zeros((nc, 2 * PHI_HIDDEN), F32)
    bm = jnp.zeros((nc, 2 * PHI_HIDDEN), F32)
    for l in range(half):
        y = x_ref[pl.ds(l, nc, stride=CMP_STRIDE), :]
        a = a + jnp.dot((y + pos_ref[l]).astype(BF16), w1_ref[l], preferred_element_type=F32)
        bm = bm + jnp.dot((y + pos_ref[half + l]).astype(BF16), w1_ref[half + l],
                          preferred_element_type=F32)
    hid = a + pltpu.roll(bm, nc - 1, 0)
    act = _gelu_exact(hid)
    o_ref[...] = jnp.dot(act.astype(BF16), w2_ref[...], preferred_element_type=F32).astype(o_ref.dtype)


def _compress(zf3, w1bd, posbd, w2bd):
    b, s, _ = zf3.shape
    nc = s // CMP_STRIDE
    return pl.pallas_call(
        _compress_kernel,
        grid=(b, NSA_GROUPS),
        in_specs=[pl.BlockSpec((None, s, LANES), lambda bi, g: (bi, 0, ZF_KCVC // LANES + g)),
                  pl.BlockSpec((CMP_BLOCK, LANES, 2 * PHI_HIDDEN), lambda bi, g: (0, 0, 0)),
                  pl.BlockSpec((CMP_BLOCK, 1, LANES), lambda bi, g: (0, 0, 0)),
                  pl.BlockSpec((2 * PHI_HIDDEN, LANES), lambda bi, g: (0, 0))],
        out_specs=pl.BlockSpec((None, None, nc, LANES), lambda bi, g: (bi, g, 0, 0)),
        out_shape=jax.ShapeDtypeStruct((b, NSA_GROUPS, nc, LANES), BF16),
        compiler_params=_cparams(("parallel", "parallel")),
        name="nsa_compress",
    )(zf3, w1bd, posbd, w2bd)


def _head_q(q128, odd):
    lane = lax.broadcasted_iota(jnp.int32, (1, LANES), 1)
    if odd:
        q128 = pltpu.roll(q128, HEAD_DIM, 1)
    return jnp.where(lane < HEAD_DIM, q128 * SCALE, 0.0).astype(BF16)


def _pair_lanes(even_hi, odd_hi):
    lane = lax.broadcasted_iota(jnp.int32, (1, LANES), 1)
    return jnp.where(lane < HEAD_DIM, pltpu.roll(even_hi, HEAD_DIM, 1), odd_hi)


def _cmp_select_kernel(q_ref, c_ref, ov_ref, ocmp_ref, bias_ref, imp_ref, *, n_sel, top_k):
    t = q_ref.shape[0]
    nc = c_ref.shape[1]
    qi = pl.program_id(1)
    t0 = qi * t
    tpos = t0 + lax.broadcasted_iota(jnp.int32, (t, 1), 0)
    ncol = lax.broadcasted_iota(jnp.int32, (1, nc), 1)
    valid = (ncol * CMP_STRIDE + (CMP_BLOCK - 1)) <= tpos
    validf = valid.astype(F32)
    q = q_ref[...].astype(F32)
    imp = jnp.zeros((t, LANES), F32)
    for g in range(NSA_GROUPS):
        ckv = c_ref[g]
        psum = jnp.zeros((t, nc), F32)
        outs = []
        for hh in range(NSA_HPG):
            h = g * NSA_HPG + hh
            qm = _head_q(q[:, (h // 2) * LANES:(h // 2 + 1) * LANES], h % 2 == 1)
            s = jnp.where(valid, _qk(qm, ckv), NEG)
            e = jnp.exp(s - jnp.max(s, axis=-1, keepdims=True))
            p = e / jnp.sum(e, axis=-1, keepdims=True) * validf
            outs.append(jnp.dot(p.astype(BF16), ckv, preferred_element_type=F32))
            psum = psum + p
        for pr in range(NSA_HPG // 2):
            c0 = (g * (NSA_HPG // 2) + pr) * LANES
            ocmp_ref[:, c0:c0 + LANES] = _pair_lanes(outs[2 * pr], outs[2 * pr + 1]).astype(ocmp_ref.dtype)
        imp = imp + jnp.dot(psum, ov_ref[g], preferred_element_type=F32,
                            precision=lax.Precision.HIGHEST)

    imp_t = imp.T
    jrow = lax.broadcasted_iota(jnp.int32, (LANES, 1), 0) % SEL_BLOCK
    cur = (t0 + lax.broadcasted_iota(jnp.int32, (1, t), 1)) // SEL_BLOCK
    forced = jnp.logical_or(jrow == 0, jnp.logical_and(jrow <= cur, jrow > cur - N_LOCAL_SEL))
    imp_t = jnp.where(forced, BIG, jnp.where(jrow > cur, NEG, imp_t))
    imp_ref[...] = imp_t

    jblk = lax.broadcasted_iota(jnp.int32, (SEL_BLOCK, 1), 0)
    bias_rows = []
    for g in range(NSA_GROUPS):
        blk = imp_t[g * SEL_BLOCK:(g + 1) * SEL_BLOCK, :]

        def body(i, rank, g=g, blk=blk):
            ri = imp_ref[pl.ds(g * SEL_BLOCK + i, 1), :]
            beats = jnp.logical_or(ri > blk, jnp.logical_and(ri == blk, i < jblk))
            return rank + beats.astype(F32)

        rank = lax.fori_loop(0, n_sel, body, jnp.zeros((SEL_BLOCK, t), F32))
        bias_rows.append(jnp.where(rank < top_k, 0.0, NEG))
    bias = jnp.concatenate(bias_rows, axis=0).T
    lane = lax.broadcasted_iota(jnp.int32, (1, LANES), 1)
    bias_ref[:, 0:LANES] = jnp.where(lane < SEL_BLOCK, bias, NEG)
    bias_ref[:, LANES:2 * LANES] = jnp.where(lane < SEL_BLOCK, pltpu.roll(bias, SEL_BLOCK, 1), NEG)


def _cmp_select(zh3, cmp_kv, ov):
    b, s, _ = zh3.shape
    t = ATT_T
    nc = cmp_kv.shape[2]
    n_sel = s // SEL_BLOCK
    kern = functools.partial(_cmp_select_kernel, n_sel=n_sel, top_k=min(SEL_TOPK, n_sel))
    return pl.pallas_call(
        kern,
        grid=(b, s // t),
        in_specs=[pl.BlockSpec((None, t, 512), lambda bi, qi: (bi, qi, ZH_NQ // 512)),
                  pl.BlockSpec((None, NSA_GROUPS, nc, LANES), lambda bi, qi: (bi, 0, 0, 0)),
                  pl.BlockSpec((NSA_GROUPS, nc, LANES), lambda bi, qi: (0, 0, 0))],
        out_specs=[pl.BlockSpec((None, t, 512), lambda bi, qi: (bi, qi, 0)),
                   pl.BlockSpec((None, t, 2 * LANES), lambda bi, qi: (bi, qi, 0))],
        out_shape=[jax.ShapeDtypeStruct((b, s, NSA_HEADS * HEAD_DIM), F32),
                   jax.ShapeDtypeStruct((b, s, 2 * LANES), F32)],
        scratch_shapes=[pltpu.VMEM((LANES, t), F32)],
        compiler_params=_cparams(("parallel", "arbitrary")),
        name="nsa_cmp_select",
    )(zh3, cmp_kv, ov)


def _sel_win_kernel(q_ref, ks_ref, kw_ref, sb_ref, oc_ref, ng_ref, o_ref, m_ref, l_ref, acc_ref):
    t = q_ref.shape[0]
    g = pl.program_id(1)
    qi = pl.program_id(2)
    lane = lax.broadcasted_iota(jnp.int32, (1, LANES), 1)
    q = q_ref[...].astype(F32)
    qh = [_head_q(q[:, (hh // 2) * LANES:(hh // 2 + 1) * LANES], hh % 2 == 1) for hh in range(NSA_HPG)]
    _online_init(m_ref, l_ref, acc_ref)
    blocks_per_tile = t // SEL_BLOCK

    def sel_step(kt, causal):
        start = pl.multiple_of(kt * t, t)
        kv = ks_ref[pl.ds(start, t), :]
        sb = pltpu.roll(sb_ref[...], (LANES - kt * blocks_per_tile) % LANES, 1)
        chunks = []
        for c in range(t // LANES):
            lo = jnp.broadcast_to(sb[:, 2 * c:2 * c + 1], (t, LANES))
            hi = jnp.broadcast_to(sb[:, 2 * c + 1:2 * c + 2], (t, LANES))
            chunks.append(jnp.where(lane < SEL_BLOCK, lo, hi))
        bias = jnp.concatenate(chunks, axis=1)
        if causal:
            row, col = _tile_iotas(t)
            bias = jnp.where(col <= row, bias, NEG)
        for hh in range(NSA_HPG):
            _online_update(_qk(qh[hh], kv) + bias, kv, m_ref, l_ref, acc_ref, hh)

    def body(kt, carry):
        sel_step(kt, False)
        return carry

    lax.fori_loop(0, qi, body, 0)
    sel_step(qi, True)

    def win_step(kt, mode):
        start = pl.multiple_of(kt * t, t)
        kv = kw_ref[pl.ds(start, t), :]
        for hh in range(NSA_HPG):
            s = _qk(qh[hh], kv)
            if mode != 1:
                row, col = _tile_iotas(t)
                s = jnp.where(col > row if mode == 0 else col <= row, s, NEG)
            _online_update(s, kv, m_ref, l_ref, acc_ref, NSA_HPG + hh)

    for w in range(WINDOW // t + 1):
        if w == WINDOW // t:
            win_step(qi, 2)
        else:
            kt = qi - WINDOW // t + w

            @pl.when(kt >= 0)
            def _(kt=kt, w=w):
                win_step(kt, 0 if w == 0 else 1)

    sg = jax.nn.sigmoid(ng_ref[...])
    sg = pltpu.roll(sg, (LANES - g * NSA_HPG * N_BRANCHES) % LANES, 1)

    def gate_pair(pr, br):
        c0 = (2 * pr) * N_BRANCHES + br
        c1 = (2 * pr + 1) * N_BRANCHES + br
        lo = jnp.broadcast_to(sg[:, c0:c0 + 1], (t, LANES))
        hi = jnp.broadcast_to(sg[:, c1:c1 + 1], (t, LANES))
        return jnp.where(lane < HEAD_DIM, lo, hi)

    for pr in range(NSA_HPG // 2):
        osel = _pair_lanes(acc_ref[2 * pr] / l_ref[2 * pr], acc_ref[2 * pr + 1] / l_ref[2 * pr + 1])
        owin = _pair_lanes(acc_ref[NSA_HPG + 2 * pr] / l_ref[NSA_HPG + 2 * pr],
                           acc_ref[NSA_HPG + 2 * pr + 1] / l_ref[NSA_HPG + 2 * pr + 1])
        ocmp = oc_ref[:, pr * LANES:(pr + 1) * LANES]
        o = gate_pair(pr, 0) * ocmp + gate_pair(pr, 1) * osel + gate_pair(pr, 2) * owin
        o_ref[:, pr * LANES:(pr + 1) * LANES] = o.astype(o_ref.dtype)


def _sel_win(zh3, zf3, selbias, ocmp):
    b, s, _ = zh3.shape
    t = ATT_T
    assert WINDOW % t == 0
    gw = NSA_HPG * HEAD_DIM
    return pl.pallas_call(
        _sel_win_kernel,
        grid=(b, NSA_GROUPS, s // t),
        in_specs=[pl.BlockSpec((None, t, gw), lambda bi, g, qi: (bi, qi, ZH_NQ // gw + g)),
                  pl.BlockSpec((None, s, LANES), lambda bi, g, qi: (bi, 0, ZH_KSVS // LANES + g)),
                  pl.BlockSpec((None, s, LANES), lambda bi, g, qi: (bi, 0, ZH_KWVW // LANES + g)),
                  pl.BlockSpec((None, t, LANES), lambda bi, g, qi: (bi, qi, g)),
                  pl.BlockSpec((None, t, gw), lambda bi, g, qi: (bi, qi, g)),
                  pl.BlockSpec((None, t, LANES), lambda bi, g, qi: (bi, qi, ZF_NG // LANES))],
        out_specs=pl.BlockSpec((None, t, gw), lambda bi, g, qi: (bi, qi, g)),
        out_shape=jax.ShapeDtypeStruct((b, s, NSA_HEADS * HEAD_DIM), BF16),
        scratch_shapes=[pltpu.VMEM((2 * NSA_HPG, t, 1), F32), pltpu.VMEM((2 * NSA_HPG, t, 1), F32),
                        pltpu.VMEM((2 * NSA_HPG, t, LANES), F32)],
        compiler_params=_cparams(("parallel", "parallel", "arbitrary")),
        name="nsa_sel_win",
    )(zh3, zh3, zh3, selbias, ocmp, zf3)


def _causal_conv3(u, prev, w_ref):
    t = u.shape[0]
    row8 = lax.broadcasted_iota(jnp.int32, (SUBLANES, 1), 0)
    out = u * w_ref[CONV_W - 1:CONV_W, :]
    for sh in range(1, CONV_W):
        rolled = pltpu.roll(u, sh, 0)
        top = jnp.where(row8 < sh, pltpu.roll(prev, sh, 0), rolled[0:SUBLANES])
        shifted = jnp.concatenate([top, rolled[SUBLANES:]], axis=0)
        out = out + shifted * w_ref[CONV_W - 1 - sh:CONV_W - sh, :]
    return out


def _layer_norm(r, g_ref, b_ref):
    mu = jnp.mean(r, axis=-1, keepdims=True)
    d = r - mu
    var = jnp.mean(d * d, axis=-1, keepdims=True)
    return d * lax.rsqrt(var + LN_EPS) * g_ref[...] + b_ref[...]


def _merge_kernel(ab_ref, ac_ref, av_ref, hc_ref, hv_ref, yb_ref, yc_ref, g0_ref, g1_ref, g2_ref,
                  x_ref, cw_ref, wb_ref, wo_ref, lg_ref, lb_ref, o_ref, *, tiles_per_seq):
    i = pl.program_id(0)
    u = ac_ref[...] * av_ref[...]
    prev = jnp.where(i % tiles_per_seq == 0, 0.0, hc_ref[...] * hv_ref[...])
    ya = ab_ref[...] * _causal_conv3(u, prev, cw_ref)
    m = jax.nn.sigmoid(g0_ref[...]) * jnp.dot(ya.astype(BF16), wb_ref[0], preferred_element_type=F32)
    m = m + jax.nn.sigmoid(g1_ref[...]) * jnp.dot(yb_ref[...], wb_ref[1], preferred_element_type=F32)
    m = m + jax.nn.sigmoid(g2_ref[...]) * jnp.dot(yc_ref[...], wb_ref[2], preferred_element_type=F32)
    mix = jnp.dot(m.astype(BF16), wo_ref[...], preferred_element_type=F32)
    o_ref[...] = _layer_norm(ALPHA * x_ref[...] + mix, lg_ref, lb_ref)


def _merge(zf2, yb2, yc2, x2, conv_w, wb, wo, lg, lb, seq, tm=256):
    t = x2.shape[0]
    cw = CONV_WIDTH
    hb = tm // SUBLANES
    kern = functools.partial(_merge_kernel, tiles_per_seq=seq // tm)

    def zcol(c):
        return pl.BlockSpec((tm, cw), lambda i: (i, c))

    def halo(c):
        return pl.BlockSpec((SUBLANES, cw), lambda i: (jnp.maximum(i * hb - 1, 0), c))

    def gate(br):
        return pl.BlockSpec((tm, D_MODEL), lambda i: (i, ZF_MG // D_MODEL + br))

    full = lambda shp: pl.BlockSpec(shp, lambda i: (0,) * len(shp))
    return pl.pallas_call(
        kern,
        grid=(t // tm,),
        in_specs=[zcol(ZF_AB // cw), zcol(ZF_AC // cw), zcol(ZF_AV // cw),
                  halo(ZF_AC // cw), halo(ZF_AV // cw),
                  pl.BlockSpec((tm, cw), lambda i: (i, 0)), pl.BlockSpec((tm, cw), lambda i: (i, 0)),
                  gate(0), gate(1), gate(2),
                  pl.BlockSpec((tm, D_MODEL), lambda i: (i, 0)),
                  full((CONV_W, cw)), full((N_BRANCHES, cw, D_MODEL)), full((D_MODEL, D_MODEL)),
                  full((1, D_MODEL)), full((1, D_MODEL))],
        out_specs=pl.BlockSpec((tm, D_MODEL), lambda i: (i, 0)),
        out_shape=jax.ShapeDtypeStruct((t, D_MODEL), F32),
        compiler_params=_cparams(("parallel",)),
        name="merge",
    )(zf2, zf2, zf2, zf2, zf2, yb2, yc2, zf2, zf2, zf2, x2, conv_w, wb, wo, lg, lb)


def _ffn_up_kernel(x_ref, xh_ref, wa_ref, wu_ref, cwa_ref, cwu_ref, ba_ref, bu_ref, o_ref, *, tiles_per_seq):
    i = pl.program_id(0)
    xb = x_ref[...].astype(BF16)
    xh = xh_ref[...].astype(BF16)
    first = i % tiles_per_seq == 0

    def branch(w_ref, cw_ref, b_ref):
        h = jnp.dot(xb, w_ref[...], preferred_element_type=F32)
        hp = jnp.where(first, 0.0, jnp.dot(xh, w_ref[...], preferred_element_type=F32))
        return _causal_conv3(h, hp, cw_ref) + b_ref[...]

    a = branch(wa_ref, cwa_ref, ba_ref)
    u = branch(wu_ref, cwu_ref, bu_ref)
    o_ref[...] = (_gelu_exact(a) * u).astype(o_ref.dtype)


def _ffn_up(x2, w_up, conv_w, conv_b, seq, tm=512, tn=256):
    t = x2.shape[0]
    nj = D_FF // tn
    hb = tm // SUBLANES
    kern = functools.partial(_ffn_up_kernel, tiles_per_seq=seq // tm)
    return pl.pallas_call(
        kern,
        grid=(t // tm, nj),
        in_specs=[pl.BlockSpec((tm, D_MODEL), lambda i, j: (i, 0)),
                  pl.BlockSpec((SUBLANES, D_MODEL), lambda i, j: (jnp.maximum(i * hb - 1, 0), 0)),
                  pl.BlockSpec((D_MODEL, tn), lambda i, j: (0, j)),
                  pl.BlockSpec((D_MODEL, tn), lambda i, j: (0, nj + j)),
                  pl.BlockSpec((CONV_W, tn), lambda i, j: (0, j)),
                  pl.BlockSpec((CONV_W, tn), lambda i, j: (0, nj + j)),
                  pl.BlockSpec((1, tn), lambda i, j: (0, j)),
                  pl.BlockSpec((1, tn), lambda i, j: (0, nj + j))],
        out_specs=pl.BlockSpec((tm, tn), lambda i, j: (i, j)),
        out_shape=jax.ShapeDtypeStruct((t, D_FF), BF16),
        compiler_params=_cparams(("parallel", "arbitrary")),
        name="ffn_up",
    )(x2, x2, w_up, w_up, conv_w, conv_w, conv_b, conv_b)


def _ffn_down_kernel(h_ref, w_ref, x_ref, lg_ref, lb_ref, o_ref):
    ffn = jnp.dot(h_ref[...], w_ref[...], preferred_element_type=F32)
    o_ref[...] = _layer_norm(ALPHA * x_ref[...] + ffn, lg_ref, lb_ref)


def _ffn_down(h2, w_down, x2, lg, lb, tm=512):
    t = x2.shape[0]
    full = lambda shp: pl.BlockSpec(shp, lambda i: (0,) * len(shp))
    return pl.pallas_call(
        _ffn_down_kernel,
        grid=(t // tm,),
        in_specs=[pl.BlockSpec((tm, D_FF), lambda i: (i, 0)), full((D_FF, D_MODEL)),
                  pl.BlockSpec((tm, D_MODEL), lambda i: (i, 0)),
                  full((1, D_MODEL)), full((1, D_MODEL))],
        out_specs=pl.BlockSpec((tm, D_MODEL), lambda i: (i, 0)),
        out_shape=jax.ShapeDtypeStruct((t, D_MODEL), F32),
        compiler_params=_cparams(("parallel",)),
        name="ffn_down",
    )(h2, w_down, x2, lg, lb)


def _rope_tables(seq):
    half = ROPE_DIMS // 2
    pos = jnp.arange(seq, dtype=F32)
    inv_freq = ROPE_THETA ** (-jnp.arange(0, ROPE_DIMS, 2, dtype=F32) / ROPE_DIMS)
    ang = pos[:, None] * inv_freq[None, :]
    cos, sin = jnp.cos(ang), jnp.sin(ang)
    ones = jnp.ones((seq, HEAD_DIM - ROPE_DIMS), F32)
    zeros8 = jnp.zeros((seq, half), F32)
    zrest = jnp.zeros((seq, HEAD_DIM - ROPE_DIMS), F32)
    c = jnp.concatenate([cos, cos, ones], axis=1)
    s1 = jnp.concatenate([-sin, zeros8, zrest], axis=1)
    s2 = jnp.concatenate([zeros8, sin, zrest], axis=1)
    rep = LANES // HEAD_DIM
    return jnp.tile(c, (1, rep)), jnp.tile(s1, (1, rep)), jnp.tile(s2, (1, rep))


def _overlap_table(seq):
    nc = seq // CMP_STRIDE
    n_sel = seq // SEL_BLOCK
    cstart = np.arange(nc) * CMP_STRIDE
    sstart = np.arange(n_sel) * SEL_BLOCK
    ov = np.maximum(np.minimum((cstart + CMP_BLOCK)[:, None], (sstart + SEL_BLOCK)[None, :])
                    - np.maximum(cstart[:, None], sstart[None, :]), 0).astype(np.float32) / CMP_BLOCK
    ov[nc - 1] = 0.0
    out = np.zeros((NSA_GROUPS, nc, LANES), np.float32)
    for g in range(NSA_GROUPS):
        out[g, :, g * SEL_BLOCK:g * SEL_BLOCK + n_sel] = ov
    return jnp.asarray(out)


_NKV0 = 3 * CONV_WIDTH + 3 * 512 + NSA_HEADS * HEAD_DIM
_NG0 = _NKV0 + 6 * NSA_GROUPS * HEAD_DIM
_MG0 = _NG0 + N_BRANCHES * NSA_HEADS


def _kv_cols(kset, vset):
    cols = []
    for g in range(NSA_GROUPS):
        for st in (kset, vset):
            base = _NKV0 + st * NSA_GROUPS * HEAD_DIM + g * HEAD_DIM
            cols.extend(range(base, base + HEAD_DIM))
    return np.asarray(cols, np.int32)


def _prep_w_in(w_in):
    d = w_in.shape[0]
    ng = w_in[:, :, _NG0:_MG0]
    ng = jnp.pad(ng, ((0, 0), (0, 0), (0, ZF_KCVC - ZF_NG - ng.shape[-1])))
    w_f = jnp.concatenate([w_in[:, :, 0:3 * CONV_WIDTH], ng, jnp.take(w_in, _kv_cols(0, 1), axis=2),
                           w_in[:, :, _MG0:]], axis=2)
    dq0, dk0, dv0, nq0 = 1536, 2048, 2560, 3072
    w_h = jnp.concatenate([w_in[:, :, dq0:dq0 + 512], w_in[:, :, dk0:dk0 + 512], w_in[:, :, nq0:nq0 + 512],
                           jnp.take(w_in, _kv_cols(2, 3), axis=2), jnp.take(w_in, _kv_cols(4, 5), axis=2),
                           w_in[:, :, dv0:dv0 + 512]], axis=2)
    assert w_f.shape == (d, D_MODEL, NF) and w_h.shape == (d, D_MODEL, NH)
    return w_f.astype(BF16), w_h.astype(BF16)


def _prep_phi(cmp_pos, phi_w1, phi_w2):
    d = phi_w1.shape[0]
    w1 = phi_w1.reshape(d, 2, CMP_BLOCK, HEAD_DIM, PHI_HIDDEN)
    z1 = jnp.zeros_like(w1[:, 0])
    w1bd = jnp.concatenate([jnp.concatenate([w1[:, 0], z1], axis=-1),
                            jnp.concatenate([z1, w1[:, 1]], axis=-1)], axis=-2)
    z2 = jnp.zeros_like(phi_w2[:, 0])
    w2bd = jnp.concatenate([jnp.concatenate([phi_w2[:, 0], z2], axis=-1),
                            jnp.concatenate([z2, phi_w2[:, 1]], axis=-1)], axis=-2)
    posbd = jnp.concatenate([cmp_pos[:, 0], cmp_pos[:, 1]], axis=-1)[:, :, None, :]
    return w1bd.astype(BF16), posbd.astype(F32), w2bd.astype(BF16)


def kernel(x, w_in, conv_a_w, diff_lambda, diff_subln, nsa_cmp_pos, nsa_phi_w1, nsa_phi_w2, w_branch, w_o,
           ln1_g, ln1_b, ffn_w_up, ffn_conv_w, ffn_conv_b, ffn_w_down, ln2_g, ln2_b):
    b, s, dm = x.shape
    depth = w_in.shape[0]
    assert dm == D_MODEL and s % 512 == 0 and s // SEL_BLOCK <= SEL_BLOCK
    ropec, ropes1, ropes2 = _rope_tables(s)
    ov = _overlap_table(s)
    w_f, w_h = _prep_w_in(w_in)
    w1bd, posbd, w2bd = _prep_phi(nsa_cmp_pos, nsa_phi_w1, nsa_phi_w2)
    wb = w_branch.astype(BF16)
    wo = w_o.astype(BF16)
    wup = ffn_w_up.astype(BF16)
    wdn = ffn_w_down.astype(BF16)
    tn = PROJ_TN
    x2 = x.reshape(b * s, dm)
    for l in range(depth):
        lam_init = 0.8 - 0.6 * math.exp(-0.3 * l)
        zf = _in_proj(x2, w_f[l], ropec, ropes1, ropes2, F32, both=(0, 0),
                      first=(ZF_KCVC // tn, ZF_MG // tn), seq=s)
        zh = _in_proj(x2, w_h[l], ropec, ropes1, ropes2, BF16, both=(0, ZH_KSVS // tn),
                      first=(ZH_KSVS // tn, ZH_DV // tn), seq=s)
        zf3 = zf.reshape(b, s, NF)
        zh3 = zh.reshape(b, s, NH)
        yb = _diff_attn(zh3, diff_lambda[l], diff_subln[l][None, :], lam_init)
        cmp_kv = _compress(zf3, w1bd[l], posbd[l], w2bd[l])
        ocmp, selbias = _cmp_select(zh3, cmp_kv, ov)
        yc = _sel_win(zh3, zf3, selbias, ocmp)
        x2 = _merge(zf, yb.reshape(b * s, -1), yc.reshape(b * s, -1), x2, conv_a_w[l], wb[l], wo[l],
                    ln1_g[l][None, :], ln1_b[l][None, :], seq=s)
        h = _ffn_up(x2, wup[l], ffn_conv_w[l], ffn_conv_b[l][None, :], seq=s)
        x2 = _ffn_down(h, wdn[l], x2, ln2_g[l][None, :], ln2_b[l][None, :])
    return x2.reshape(b, s, dm)
```
